```python
import math
import jax, jax.numpy as jnp
from jax import lax
import numpy as np

D_MODEL = 2048
BATCH = 2
SEQ = 4096
DEPTH = 4
DEC_BATCH = 8
DEC_SEQ = 1
PAST_LEN = 16384
PAGE_SIZE = 128

MIX_WIDTH = D_MODEL
GROUP_WIDTH = MIX_WIDTH // 4
A_HEADS = 4
A_HEAD_DIM = GROUP_WIDTH // A_HEADS
MOBA_BLOCK = 256
MOBA_TOPK = 3
MOBA_QCHUNK = 32
B_HEADS = 4
B_V_DIM = GROUP_WIDTH // B_HEADS
B_QK_DIM = B_V_DIM // 2
B_QK_HEADS = 2 * B_HEADS
ATTN_QBLOCK = 128
SUBLN_EPS = 1e-5
C_GROUPS = 4
C_GROUP_DIM = GROUP_WIDTH // C_GROUPS
C_CHUNK = 128
LN_EPS = 1e-5
D_GROUP_CH = 16
D_GROUPS = GROUP_WIDTH // D_GROUP_CH
D_STATE = 64
FFN_HIDDEN = ((8 * D_MODEL + 3 * 256 - 1) // (3 * 256)) * 256
ROPE_THETA = 10000.0
RMS_EPS = 1e-6
N_IN = 9 * GROUP_WIDTH
IN_SPLITS = (GROUP_WIDTH, 2 * GROUP_WIDTH, 3 * GROUP_WIDTH, 4 * GROUP_WIDTH,
             5 * GROUP_WIDTH, 6 * GROUP_WIDTH, 8 * GROUP_WIDTH)

kernel_name = "hymba_moba_diff_gmlp_s5_decode_step"

F32 = jnp.float32


def rmsnorm(x, g, eps=RMS_EPS):
    xf = x.astype(F32)
    y = xf * lax.rsqrt(jnp.mean(xf * xf, axis=-1, keepdims=True) + eps)
    return (y * g.astype(F32)).astype(x.dtype)


def layernorm(x, g, b, eps=LN_EPS):
    xf = x.astype(F32)
    mu = jnp.mean(xf, axis=-1, keepdims=True)
    xc = xf - mu
    y = xc * lax.rsqrt(jnp.mean(xc * xc, axis=-1, keepdims=True) + eps)
    return (y * g.astype(F32) + b.astype(F32)).astype(x.dtype)


def rope(x, pos):
    d = x.shape[-1]
    half = d // 2
    inv = jnp.power(jnp.float32(ROPE_THETA), -jnp.arange(half, dtype=F32) * 2.0 / d)
    ang = pos.astype(F32)[:, None] * inv[None, :]
    cos = jnp.cos(ang)[:, None, :]
    sin = jnp.sin(ang)[:, None, :]
    xf = x.astype(F32)
    x1, x2 = xf[..., :half], xf[..., half:]
    return jnp.concatenate([x1 * cos - x2 * sin, x2 * cos + x1 * sin], axis=-1).astype(x.dtype)


def gather_pages(pool, page_table):
    rows = pool[page_table]
    return rows.reshape((rows.shape[0], rows.shape[1] * rows.shape[2]) + rows.shape[3:])


def moba_chunk(qc, pc, kb, vb, kmean):
    nb = kb.shape[2]
    hd = qc.shape[-1]
    cur = pc // MOBA_BLOCK
    gate = jnp.einsum('bhqd,bhnd->bhqn', qc.astype(F32), kmean)
    past = jnp.arange(nb)[None, :] < cur[:, None]
    gate = jnp.where(past[None, None], gate, -jnp.inf)
    _, top = lax.top_k(gate, MOBA_TOPK)
    own = jnp.broadcast_to(cur[:, None].astype(top.dtype), top.shape[:-1] + (1,))
    idx = jnp.concatenate([top, own], axis=-1)
    gather = jax.vmap(jax.vmap(lambda blocks, ix: blocks[ix]))
    kg = gather(kb, idx)
    vg = gather(vb, idx)
    logits = jnp.einsum('bhqd,bhqrkd->bhqrk', qc.astype(F32), kg.astype(F32)) * (1.0 / math.sqrt(hd))
    key_pos = idx[..., None] * MOBA_BLOCK + jnp.arange(MOBA_BLOCK)
    sel_ok = jnp.concatenate(
        [jnp.arange(MOBA_TOPK)[None, :] < jnp.minimum(cur, MOBA_TOPK)[:, None],
         jnp.ones((cur.shape[0], 1), dtype=bool)], axis=-1)
    mask = sel_ok[None, None, :, :, None] & (key_pos <= pc[None, None, :, None, None])
    logits = jnp.where(mask, logits, -jnp.inf)
    bn, hn, qn = logits.shape[:3]
    p = jax.nn.softmax(logits.reshape(bn, hn, qn, -1), axis=-1).reshape(logits.shape)
    return jnp.einsum('bhqrk,bhqrkd->bhqd', p, vg.astype(F32))


def moba_attend(q, k, v, q_pos):
    bn, tk, hn, hd = k.shape
    nb = max(-(-tk // MOBA_BLOCK), MOBA_TOPK)
    pad = nb * MOBA_BLOCK - tk
    k = jnp.pad(k, ((0, 0), (0, pad), (0, 0), (0, 0)))
    v = jnp.pad(v, ((0, 0), (0, pad), (0, 0), (0, 0)))
    kb = k.transpose(0, 2, 1, 3).reshape(bn, hn, nb, MOBA_BLOCK, hd)
    vb = v.transpose(0, 2, 1, 3).reshape(bn, hn, nb, MOBA_BLOCK, hd)
    kmean = jnp.mean(kb.astype(F32), axis=3)
    qh = q.transpose(0, 2, 1, 3)
    nq = qh.shape[2]
    if nq > MOBA_QCHUNK and nq % MOBA_QCHUNK == 0:
        n = nq // MOBA_QCHUNK
        qs = qh.reshape(bn, hn, n, MOBA_QCHUNK, hd).transpose(2, 0, 1, 3, 4)
        ps = q_pos.reshape(n, MOBA_QCHUNK)
        out = lax.map(lambda a: moba_chunk(a[0], a[1], kb, vb, kmean), (qs, ps))
        out = out.transpose(1, 2, 0, 3, 4).reshape(bn, hn, nq, hd)
    else:
        out = moba_chunk(qh, q_pos, kb, vb, kmean)
    return out.transpose(0, 2, 1, 3).reshape(bn, nq, hn * hd)


def diff_block(qb, pb, k, v, kpos, lam):
    s = jnp.einsum('bhqd,bhkd->bhqk', qb.astype(F32), k.astype(F32)) * (1.0 / math.sqrt(B_QK_DIM))
    s = jnp.where((kpos[None, :] <= pb[:, None])[None, None], s, -jnp.inf)
    p = jax.nn.softmax(s, axis=-1)
    bn, _, qn, tk = p.shape
    p = p.reshape(bn, B_HEADS, 2, qn, tk)
    w = p[:, :, 0] - lam * p[:, :, 1]
    return jnp.einsum('bhqk,bhkd->bhqd', w, v.astype(F32))


def diff_attend(q, k, v, q_pos, lam, lam_init, subln_g):
    bn, nq = q.shape[:2]
    tk = k.shape[1]
    qh = q.transpose(0, 2, 1, 3)
    kh = k.transpose(0, 2, 1, 3)
    vh = v.transpose(0, 2, 1, 3)
    kpos = jnp.arange(tk, dtype=jnp.int32)
    if nq > ATTN_QBLOCK and nq % ATTN_QBLOCK == 0:
        n = nq // ATTN_QBLOCK
        qs = qh.reshape(bn, B_QK_HEADS, n, ATTN_QBLOCK, B_QK_DIM).transpose(2, 0, 1, 3, 4)
        ps = q_pos.reshape(n, ATTN_QBLOCK)
        out = lax.map(lambda a: diff_block(a[0], a[1], kh, vh, kpos, lam), (qs, ps))
        out = out.transpose(1, 2, 0, 3, 4).reshape(bn, B_HEADS, nq, B_V_DIM)
    else:
        out = diff_block(qh, q_pos, kh, vh, kpos, lam)
    out = out * lax.rsqrt(jnp.mean(out * out, axis=-1, keepdims=True) + SUBLN_EPS)
    out = out * subln_g.astype(F32) * (1.0 - lam_init)
    return out.transpose(0, 2, 1, 3).reshape(bn, nq, B_HEADS * B_V_DIM)


def gmlp_mix(z, ln_g, ln_b, w_s, b_s):
    z = jax.nn.gelu(z, approximate=False)
    u, vv = z[..., :GROUP_WIDTH], z[..., GROUP_WIDTH:]
    vv = layernorm(vv, ln_g, ln_b)
    bn, t = vv.shape[:2]
    L = t if t <= C_CHUNK else C_CHUNK
    vc = vv.reshape(bn, t // L, L, C_GROUPS, C_GROUP_DIM)
    causal = jnp.tril(jnp.ones((L, L), dtype=bool))
    w = jnp.where(causal[None], w_s[:, :L, :L], 0.0)
    mixed = jnp.einsum('gts,bcsgd->bctgd', w, vc) + b_s[:, :L].T[None, None, :, :, None]
    return (u * mixed.reshape(bn, t, GROUP_WIDTH)).astype(z.dtype), vv


def s5_mix(u, h0, a_re, a_im, b_re, b_im, c_re, c_im, d_skip, log_step, glu_w, glu_b):
    bn, t = u.shape[:2]
    lam = lax.complex(a_re.astype(F32), a_im.astype(F32))
    step = jnp.exp(log_step.astype(F32))[:, None]
    a_bar = jnp.exp(lam * step)
    b_bar = ((a_bar - 1.0) / lam)[..., None] * lax.complex(b_re.astype(F32), b_im.astype(F32))
    c_mat = lax.complex(c_re.astype(F32), c_im.astype(F32))
    ug = u.astype(F32).reshape(bn, t, D_GROUPS, D_GROUP_CH)
    bu = jnp.einsum('gnc,btgc->btgn', b_bar, ug.astype(jnp.complex64))
    a_seq = jnp.broadcast_to(a_bar, bu.shape)

    def combine(left, right):
        a_l, b_l = left
        a_r, b_r = right
        return a_r * a_l, a_r * b_l + b_r

    a_cum, b_cum = lax.associative_scan(combine, (a_seq, bu), axis=1)
    h = a_cum * h0[:, None] + b_cum
    y = jnp.einsum('gcn,btgn->btgc', c_mat, h).real + d_skip.astype(F32).reshape(D_GROUPS, D_GROUP_CH) * ug
    y = jax.nn.gelu(y.reshape(bn, t, GROUP_WIDTH), approximate=False)
    y = y * jax.nn.sigmoid(y @ glu_w.astype(F32) + glu_b.astype(F32))
    return y.astype(u.dtype), h[:, -1]


def hybrid_layer(x, pos, past_ak, past_av, past_bk, past_bv, h0, lam_init,
                 g_mix, w_in, w_out, lq1, lk1, lq2, lk2, subln_g, ln_g, ln_b, w_s, b_s,
                 a_re, a_im, b_re, b_im, c_re, c_im, d_skip, log_step, glu_w, glu_b,
                 g_ffn, w_gate, w_up, w_down):
    bn, t, _ = x.shape
    h = rmsnorm(x, g_mix)
    z = h @ w_in
    aq, ak, av, bq, bk, bv, cz, du = jnp.split(z, IN_SPLITS, axis=-1)
    aq = rope(aq.reshape(bn, t, A_HEADS, A_HEAD_DIM), pos)
    ak = rope(ak.reshape(bn, t, A_HEADS, A_HEAD_DIM), pos)
    av = av.reshape(bn, t, A_HEADS, A_HEAD_DIM)
    ka = ak if past_ak is None else jnp.concatenate([past_ak.astype(ak.dtype), ak], axis=1)
    va = av if past_av is None else jnp.concatenate([past_av.astype(av.dtype), av], axis=1)
    a_out = moba_attend(aq, ka, va, pos).astype(x.dtype)
    bq = rope(bq.reshape(bn, t, B_QK_HEADS, B_QK_DIM), pos)
    bk = rope(bk.reshape(bn, t, B_QK_HEADS, B_QK_DIM), pos)
    bv = bv.reshape(bn, t, B_HEADS, B_V_DIM)
    kb_all = bk if past_bk is None else jnp.concatenate([past_bk.astype(bk.dtype), bk], axis=1)
    vb_all = bv if past_bv is None else jnp.concatenate([past_bv.astype(bv.dtype), bv], axis=1)
    lam = (jnp.exp(jnp.sum(lq1.astype(F32) * lk1.astype(F32)))
           - jnp.exp(jnp.sum(lq2.astype(F32) * lk2.astype(F32))) + lam_init)
    b_out = diff_attend(bq, kb_all, vb_all, pos, lam, lam_init, subln_g).astype(x.dtype)
    c_out, c_v = gmlp_mix(cz, ln_g, ln_b, w_s, b_s)
    if h0 is None:
        h0 = jnp.zeros((bn, D_GROUPS, D_STATE), dtype=jnp.complex64)
    d_out, h_last = s5_mix(du, h0, a_re, a_im, b_re, b_im, c_re, c_im, d_skip, log_step, glu_w, glu_b)
    mix = jnp.concatenate([a_out, b_out, c_out.astype(x.dtype), d_out], axis=-1)
    x = x + (mix @ w_out).astype(x.dtype)
    h2 = rmsnorm(x, g_ffn)
    x = x + ((jax.nn.silu(h2 @ w_gate) * (h2 @ w_up)) @ w_down).astype(x.dtype)
    return x, ak, av, bk, bv, c_v, h_last


def setup_inputs(seed: int = 0) -> dict:
    key = jax.random.key(seed)
    keys = jax.random.split(key, 40)
    counter = [0]

    def nxt():
        k = keys[counter[0]]
        counter[0] += 1
        return k

    def normal(shape, scale):
        return scale * jax.random.normal(nxt(), shape, F32)

    n_pages = PAST_LEN // PAGE_SIZE
    n_used = DEC_BATCH * n_pages
    n_pool = n_used + max(1, n_used // 4)
    perm = jax.random.permutation(nxt(), n_pool)
    page_table = perm[:n_used].reshape(DEC_BATCH, n_pages).astype(jnp.int32)

    inp = {}
    inp["x_prompt"] = normal((BATCH, SEQ, D_MODEL), 1.0)
    inp["x_sample"] = normal((DEC_BATCH, DEC_SEQ, D_MODEL), 1.0)
    inp["cache_a_k"] = normal((DEPTH, n_pool, PAGE_SIZE, A_HEADS, A_HEAD_DIM), 1.0)
    inp["cache_a_v"] = normal((DEPTH, n_pool, PAGE_SIZE, A_HEADS, A_HEAD_DIM), 1.0)
    inp["cache_b_k"] = normal((DEPTH, n_pool, PAGE_SIZE, B_QK_HEADS, B_QK_DIM), 1.0)
    inp["cache_b_v"] = normal((DEPTH, n_pool, PAGE_SIZE, B_HEADS, B_V_DIM), 1.0)
    inp["state_ssm"] = normal((DEPTH, DEC_BATCH, D_GROUPS, D_STATE, 2), 0.3)
    inp["page_table"] = page_table
    inp["norm_mix_g"] = 1.0 + normal((DEPTH, D_MODEL), 0.02)
    inp["w_in"] = normal((DEPTH, D_MODEL, N_IN), D_MODEL ** -0.5)
    inp["w_out"] = normal((DEPTH, MIX_WIDTH, D_MODEL), MIX_WIDTH ** -0.5)
    inp["diff_lam_q1"] = normal((DEPTH, B_QK_DIM), 0.1)
    inp["diff_lam_k1"] = normal((DEPTH, B_QK_DIM), 0.1)
    inp["diff_lam_q2"] = normal((DEPTH, B_QK_DIM), 0.1)
    inp["diff_lam_k2"] = normal((DEPTH, B_QK_DIM), 0.1)
    inp["diff_subln_g"] = 1.0 + normal((DEPTH, B_V_DIM), 0.02)
    inp["gmlp_ln_g"] = 1.0 + normal((DEPTH, GROUP_WIDTH), 0.02)
    inp["gmlp_ln_b"] = normal((DEPTH, GROUP_WIDTH), 0.02)
    inp["gmlp_w_s"] = normal((DEPTH, C_GROUPS, C_CHUNK, C_CHUNK), C_CHUNK ** -0.5)
    inp["gmlp_b_s"] = 1.0 + normal((DEPTH, C_GROUPS, C_CHUNK), 0.1)
    inp["ssm_a_re"] = -0.5 + normal((DEPTH, D_GROUPS, D_STATE), 0.01)
    inp["ssm_a_im"] = jnp.pi * jnp.arange(D_STATE, dtype=F32) + normal((DEPTH, D_GROUPS, D_STATE), 0.01)
    inp["ssm_b_re"] = normal((DEPTH, D_GROUPS, D_STATE, D_GROUP_CH), (2.0 * D_GROUP_CH) ** -0.5)
    inp["ssm_b_im"] = normal((DEPTH, D_GROUPS, D_STATE, D_GROUP_CH), (2.0 * D_GROUP_CH) ** -0.5)
    inp["ssm_c_re"] = normal((DEPTH, D_GROUPS, D_GROUP_CH, D_STATE), (2.0 * D_STATE) ** -0.5)
    inp["ssm_c_im"] = normal((DEPTH, D_GROUPS, D_GROUP_CH, D_STATE), (2.0 * D_STATE) ** -0.5)
    inp["ssm_d"] = normal((DEPTH, GROUP_WIDTH), 1.0)
    inp["ssm_log_step"] = jax.random.uniform(nxt(), (DEPTH, D_GROUPS), F32, math.log(1e-3), math.log(1e-1))
    inp["glu_w"] = normal((DEPTH, GROUP_WIDTH, GROUP_WIDTH), GROUP_WIDTH ** -0.5)
    inp["glu_b"] = normal((DEPTH, GROUP_WIDTH), 0.02)
    inp["norm_ffn_g"] = 1.0 + normal((DEPTH, D_MODEL), 0.02)
    inp["ffn_w_gate"] = normal((DEPTH, D_MODEL, FFN_HIDDEN), D_MODEL ** -0.5)
    inp["ffn_w_up"] = normal((DEPTH, D_MODEL, FFN_HIDDEN), D_MODEL ** -0.5)
    inp["ffn_w_down"] = normal((DEPTH, FFN_HIDDEN, D_MODEL), FFN_HIDDEN ** -0.5)
    inp["final_norm_g"] = 1.0 + normal((D_MODEL,), 0.02)
    return inp


def reference(x_prompt, x_sample, cache_a_k, cache_a_v, cache_b_k, cache_b_v, state_ssm, page_table,
              norm_mix_g, w_in, w_out, diff_lam_q1, diff_lam_k1, diff_lam_q2, diff_lam_k2, diff_subln_g,
              gmlp_ln_g, gmlp_ln_b, gmlp_w_s, gmlp_b_s, ssm_a_re, ssm_a_im, ssm_b_re, ssm_b_im,
              ssm_c_re, ssm_c_im, ssm_d, ssm_log_step, glu_w, glu_b, norm_ffn_g, ffn_w_gate,
              ffn_w_up, ffn_w_down, final_norm_g):
    pos_p = jnp.arange(x_prompt.shape[1], dtype=jnp.int32)
    past_len = page_table.shape[1] * cache_a_k.shape[2]
    pos_s = past_len + jnp.arange(x_sample.shape[1], dtype=jnp.int32)
    xp, xs = x_prompt, x_sample
    ak_p, av_p, ak_s, av_s = [], [], [], []
    bk_p, bv_p, bk_s, bv_s = [], [], [], []
    ssm_p, ssm_s, cv_s = [], [], []
    for i in range(DEPTH):
        lam_init = 0.8 - 0.6 * math.exp(-0.3 * i)
        wts = (norm_mix_g[i], w_in[i], w_out[i], diff_lam_q1[i], diff_lam_k1[i], diff_lam_q2[i],
               diff_lam_k2[i], diff_subln_g[i], gmlp_ln_g[i], gmlp_ln_b[i], gmlp_w_s[i], gmlp_b_s[i],
               ssm_a_re[i], ssm_a_im[i], ssm_b_re[i], ssm_b_im[i], ssm_c_re[i], ssm_c_im[i], ssm_d[i],
               ssm_log_step[i], glu_w[i], glu_b[i], norm_ffn_g[i], ffn_w_gate[i], ffn_w_up[i],
               ffn_w_down[i])
        xp, a_k, a_v, b_k, b_v, _, h_last = hybrid_layer(
            xp, pos_p, None, None, None, None, None, lam_init, *wts)
        ak_p.append(a_k)
        av_p.append(a_v)
        bk_p.append(b_k)
        bv_p.append(b_v)
        ssm_p.append(jnp.stack([h_last.real, h_last.imag], axis=-1))
        h0 = lax.complex(state_ssm[i, ..., 0].astype(F32), state_ssm[i, ..., 1].astype(F32))
        xs, a_k, a_v, b_k, b_v, c_v, h_last = hybrid_layer(
            xs, pos_s,
            gather_pages(cache_a_k[i], page_table), gather_pages(cache_a_v[i], page_table),
            gather_pages(cache_b_k[i], page_table), gather_pages(cache_b_v[i], page_table),
            h0, lam_init, *wts)
        ak_s.append(a_k)
        av_s.append(a_v)
        bk_s.append(b_k)
        bv_s.append(b_v)
        cv_s.append(c_v)
        ssm_s.append(jnp.stack([h_last.real, h_last.imag], axis=-1))
    y_prompt = rmsnorm(xp, final_norm_g)
    y_sample = rmsnorm(xs, final_norm_g)
    return (y_prompt, y_sample,
            jnp.stack(ak_p), jnp.stack(av_p), jnp.stack(ak_s), jnp.stack(av_s),
            jnp.stack(bk_p), jnp.stack(bv_p), jnp.stack(bk_s), jnp.stack(bv_s),
            jnp.stack(ssm_p), jnp.stack(ssm_s), jnp.stack(cv_s))
```

```python
import functools
import math

import jax
import jax.numpy as jnp
from jax import lax
from jax.experimental import pallas as pl
from jax.experimental.pallas import tpu as pltpu

F32 = jnp.float32
BF16 = jnp.bfloat16
HIGHEST = lax.Precision.HIGHEST

RMS_EPS = 1e-6
SUBLN_EPS = 1e-5
LN_EPS = 1e-5
ROPE_THETA = 10000.0
MOBA_BLOCK = 256
MOBA_TOPK = 3
LANES = 128
SUBLANES = 8
NEG = -1e30
VMEM_LIMIT = 56 * 1024 * 1024

_NT = (((1,), (1,)), ((), ()))


def _params(sem, vmem=VMEM_LIMIT):
    return pltpu.CompilerParams(dimension_semantics=sem, vmem_limit_bytes=vmem)


def _gelu(x):
    return 0.5 * x * (1.0 + lax.erf(x * (1.0 / math.sqrt(2.0))))


def _sigmoid(x):
    return 1.0 / (1.0 + jnp.exp(-x))


def _rope_tile(z, cos, sin, half):
    if 2 * half == LANES:
        rot = pltpu.roll(z, half, axis=1)
    else:
        lane = lax.broadcasted_iota(jnp.int32, z.shape, 1)
        rot = jnp.where((lane % (2 * half)) < half,
                        pltpu.roll(z, LANES - half, axis=1), pltpu.roll(z, half, axis=1))
    return z * cos + rot * sin


def _proj_in_kernel(x_ref, g_ref, w_ref, ca_ref, sa_ref, cb_ref, sb_ref, *out_refs, gw, a_half, b_half):
    x = x_ref[...]
    ms = jnp.mean(x * x, axis=-1, keepdims=True)
    h = (x * lax.rsqrt(ms + RMS_EPS) * g_ref[...]).astype(BF16)
    for j, o_ref in enumerate(out_refs):
        z = jnp.dot(h, w_ref[:, j * gw:(j + 1) * gw], preferred_element_type=F32)
        if j in (0, 1, 3, 4):
            if j < 2:
                cos, sin, half = ca_ref[...], sa_ref[...], a_half
            else:
                cos, sin, half = cb_ref[...], sb_ref[...], b_half
            for k in range(gw // LANES):
                sl = slice(k * LANES, (k + 1) * LANES)
                o_ref[:, sl] = _rope_tile(z[:, sl], cos, sin, half)
        else:
            o_ref[...] = z


def _proj_in(x2d, g, w_bf, tabs, tm, gw, a_half, b_half):
    m, d = x2d.shape
    n = w_bf.shape[1]
    n_out = n // gw
    nt = tabs[0].shape[0] // tm
    tab_spec = pl.BlockSpec((tm, LANES), lambda i: (i % nt, 0))
    return pl.pallas_call(
        functools.partial(_proj_in_kernel, gw=gw, a_half=a_half, b_half=b_half),
        grid=(m // tm,),
        in_specs=[pl.BlockSpec((tm, d), lambda i: (i, 0)),
                  pl.BlockSpec((1, d), lambda i: (0, 0)),
                  pl.BlockSpec((d, n), lambda i: (0, 0), pipeline_mode=pl.Buffered(1)),
                  tab_spec, tab_spec, tab_spec, tab_spec],
        out_specs=[pl.BlockSpec((tm, gw), lambda i: (i, 0))] * n_out,
        out_shape=[jax.ShapeDtypeStruct((m, gw), F32)] * n_out,
        compiler_params=_params(("parallel",)),
        name="proj_in",
    )(x2d, g, w_bf, *tabs)


def _moba_kernel(q_ref, k_ref, v_ref, o_ref, ka_ref, vb_ref, km_ref, *, tq):
    t, hd = q_ref.shape[1], q_ref.shape[2]
    nb = t // MOBA_BLOCK
    nbp = km_ref.shape[0]
    scale = 1.0 / math.sqrt(hd)
    k = k_ref[0]
    ka_ref[:, :hd] = k.astype(BF16)
    row_blk = lax.broadcasted_iota(jnp.int32, (t, LANES), 0) // MOBA_BLOCK
    lane = lax.broadcasted_iota(jnp.int32, (t, LANES), 1)
    ka_ref[:, hd:] = jnp.where(row_blk == lane, 1.0, 0.0).astype(BF16)
    vb_ref[...] = v_ref[0].astype(BF16)
    km_ref[...] = jnp.zeros(km_ref.shape, F32)
    km_ref[0:nb, :] = jnp.mean(k.reshape(nb, MOBA_BLOCK, hd), axis=1)

    def q_body(i, carry):
        r0 = pl.multiple_of(i * tq, tq)
        q = q_ref[0, pl.ds(r0, tq), :]
        c = r0 // MOBA_BLOCK
        off = r0 - c * MOBA_BLOCK
        g_t = lax.dot_general(km_ref[...], q, _NT, precision=HIGHEST, preferred_element_type=F32)
        blk = lax.broadcasted_iota(jnp.int32, (nbp, tq), 0)
        valid = blk < c
        g_t = jnp.where(valid, g_t, -jnp.inf)
        rank = jnp.zeros((nbp, tq), jnp.int32)
        for m in range(nb):
            gm = g_t[m:m + 1, :]
            beats = (gm > g_t) | ((gm == g_t) & (blk > m))
            rank = rank + beats.astype(jnp.int32)
        bias_t = jnp.where(valid & (rank < MOBA_TOPK), 0.0, NEG)
        bias_t = jnp.concatenate([bias_t, jnp.zeros((LANES - nbp, tq), F32)], axis=0)
        bias = bias_t.T
        qs = (q * scale).astype(BF16)
        qa = jnp.concatenate([qs, bias.astype(BF16)], axis=1)

        k0 = pl.multiple_of(c * MOBA_BLOCK, MOBA_BLOCK)
        s = lax.dot_general(qs, ka_ref[pl.ds(k0, MOBA_BLOCK), 0:hd], _NT, preferred_element_type=F32)
        rpos = off + lax.broadcasted_iota(jnp.int32, (tq, MOBA_BLOCK), 0)
        cpos = lax.broadcasted_iota(jnp.int32, (tq, MOBA_BLOCK), 1)
        s = jnp.where(cpos <= rpos, s, NEG)
        m0 = jnp.max(s, axis=1, keepdims=True)
        p = jnp.exp(s - m0)
        l0 = jnp.sum(p, axis=1, keepdims=True)
        acc0 = jnp.dot(p.astype(BF16), vb_ref[pl.ds(k0, MOBA_BLOCK), :], preferred_element_type=F32)

        def blk_body(n, st):
            m_i, l_i, acc = st
            kn = pl.multiple_of(n * MOBA_BLOCK, MOBA_BLOCK)
            s = lax.dot_general(qa, ka_ref[pl.ds(kn, MOBA_BLOCK), :], _NT, preferred_element_type=F32)
            m_new = jnp.maximum(m_i, jnp.max(s, axis=1, keepdims=True))
            alpha = jnp.exp(m_i - m_new)
            p = jnp.exp(s - m_new)
            l_new = alpha * l_i + jnp.sum(p, axis=1, keepdims=True)
            acc = alpha * acc + jnp.dot(p.astype(BF16), vb_ref[pl.ds(kn, MOBA_BLOCK), :],
                                        preferred_element_type=F32)
            return m_new, l_new, acc

        _, l_f, acc_f = lax.fori_loop(0, c, blk_body, (m0, l0, acc0))
        o_ref[0, pl.ds(r0, tq), :] = (acc_f / l_f).astype(o_ref.dtype)
        return carry

    lax.fori_loop(0, t // tq, q_body, 0)


def _moba_prompt(q, k, v, heads, tq=128):
    b, t, w = q.shape
    hd = w // heads
    nb = t // MOBA_BLOCK
    nbp = -(-nb // SUBLANES) * SUBLANES
    assert t % MOBA_BLOCK == 0 and nbp <= LANES and hd == LANES
    spec = pl.BlockSpec((1, t, hd), lambda bi, hi: (bi, 0, hi))
    return pl.pallas_call(
        functools.partial(_moba_kernel, tq=tq),
        grid=(b, heads),
        in_specs=[spec, spec, spec],
        out_specs=spec,
        out_shape=jax.ShapeDtypeStruct((b, t, w), BF16),
        scratch_shapes=[pltpu.VMEM((t, hd + LANES), BF16), pltpu.VMEM((t, hd), BF16),
                        pltpu.VMEM((nbp, hd), F32)],
        compiler_params=_params(("parallel", "parallel")),
        name="moba_prompt",
    )(q, k, v)


def _diff_lambda(l_ref, li_ref):
    lp = l_ref[...]
    s1 = jnp.sum(lp[0:1] * lp[1:2], axis=1, keepdims=True)
    s2 = jnp.sum(lp[2:3] * lp[3:4], axis=1, keepdims=True)
    return jnp.exp(s1) - jnp.exp(s2) + li_ref[...]


def _subln(o, g, li):
    o = o * lax.rsqrt(jnp.mean(o * o, axis=-1, keepdims=True) + SUBLN_EPS)
    return o * g * (1.0 - li)


def _diff_kernel(l_ref, li_ref, g_ref, q_ref, k_ref, v_ref, o_ref, kb_ref, vb_ref, *, tq, tk):
    t, hd = q_ref.shape[1], q_ref.shape[2]
    half = hd // 2
    scale = 1.0 / math.sqrt(half)
    kb_ref[...] = k_ref[0].astype(BF16)
    vb_ref[...] = v_ref[0].astype(BF16)
    lam = _diff_lambda(l_ref, li_ref)
    li = li_ref[...]
    g = g_ref[...]
    lane = lax.broadcasted_iota(jnp.int32, (tq, hd), 1)

    def q_body(i, carry):
        r0 = pl.multiple_of(i * tq, tq)
        q = q_ref[0, pl.ds(r0, tq), :] * scale
        qq = jnp.concatenate([jnp.where(lane < half, q, 0.0), jnp.where(lane >= half, q, 0.0)],
                             axis=0).astype(BF16)
        c = r0 // tk
        off = r0 - c * tk
        k0 = pl.multiple_of(c * tk, tk)
        s = lax.dot_general(qq, kb_ref[pl.ds(k0, tk), :], _NT, preferred_element_type=F32)
        rpos = off + lax.broadcasted_iota(jnp.int32, (2 * tq, tk), 0) % tq
        cpos = lax.broadcasted_iota(jnp.int32, (2 * tq, tk), 1)
        s = jnp.where(cpos <= rpos, s, NEG)
        m0 = jnp.max(s, axis=1, keepdims=True)
        p = jnp.exp(s - m0)
        l0 = jnp.sum(p, axis=1, keepdims=True)
        acc0 = jnp.dot(p.astype(BF16), vb_ref[pl.ds(k0, tk), :], preferred_element_type=F32)

        def blk_body(n, st):
            m_i, l_i, acc = st
            kn = pl.multiple_of(n * tk, tk)
            s = lax.dot_general(qq, kb_ref[pl.ds(kn, tk), :], _NT, preferred_element_type=F32)
            m_new = jnp.maximum(m_i, jnp.max(s, axis=1, keepdims=True))
            alpha = jnp.exp(m_i - m_new)
            p = jnp.exp(s - m_new)
            l_new = alpha * l_i + jnp.sum(p, axis=1, keepdims=True)
            acc = alpha * acc + jnp.dot(p.astype(BF16), vb_ref[pl.ds(kn, tk), :],
                                        preferred_element_type=F32)
            return m_new, l_new, acc

        _, l_f, acc_f = lax.fori_loop(0, c, blk_body, (m0, l0, acc0))
        o = acc_f / l_f
        o = o[:tq] - lam * o[tq:]
        o_ref[0, pl.ds(r0, tq), :] = _subln(o, g, li).astype(o_ref.dtype)
        return carry

    lax.fori_loop(0, t // tq, q_body, 0)


def _diff_prompt(lam_p, lam_init, subln_g, q, k, v, heads, tq=128, tk=256):
    b, t, w = q.shape
    hd = w // heads
    assert t % tk == 0 and tk % tq == 0 and hd == LANES
    spec = pl.BlockSpec((1, t, hd), lambda bi, hi: (bi, 0, hi))
    return pl.pallas_call(
        functools.partial(_diff_kernel, tq=tq, tk=tk),
        grid=(b, heads),
        in_specs=[pl.BlockSpec(lam_p.shape, lambda bi, hi: (0, 0)),
                  pl.BlockSpec((1, 1), lambda bi, hi: (0, 0)),
                  pl.BlockSpec((1, hd), lambda bi, hi: (0, 0)),
                  spec, spec, spec],
        out_specs=spec,
        out_shape=jax.ShapeDtypeStruct((b, t, w), BF16),
        scratch_shapes=[pltpu.VMEM((t, hd), BF16), pltpu.VMEM((t, hd), BF16)],
        compiler_params=_params(("parallel", "parallel")),
        name="diff_prompt",
    )(lam_p, lam_init, subln_g, q, k, v)


def _layernorm(v, g, b):
    mu = jnp.mean(v, axis=-1, keepdims=True)
    vc = v - mu
    return vc * lax.rsqrt(jnp.mean(vc * vc, axis=-1, keepdims=True) + LN_EPS) * g + b


def _gmlp_kernel(u_ref, v_ref, lg_ref, lb_ref, w_ref, bt_ref, o_ref):
    tm, gwid = u_ref.shape
    ng, ch = w_ref.shape[0], w_ref.shape[1]
    cg = gwid // ng
    u = _gelu(u_ref[...])
    vn = _layernorm(_gelu(v_ref[...]), lg_ref[...], lb_ref[...])
    r = lax.broadcasted_iota(jnp.int32, (ch, ch), 0)
    c = lax.broadcasted_iota(jnp.int32, (ch, ch), 1)
    for gi in range(ng):
        wg = jnp.where(r >= c, w_ref[gi], 0.0).astype(BF16)
        bcol = bt_ref[:, gi:gi + 1]
        for ci in range(tm // ch):
            rs, cs = slice(ci * ch, (ci + 1) * ch), slice(gi * cg, (gi + 1) * cg)
            mixed = jnp.dot(wg, vn[rs, cs].astype(BF16), preferred_element_type=F32) + bcol
            o_ref[rs, cs] = (u[rs, cs] * mixed).astype(o_ref.dtype)


def _gmlp_prompt(cu, cv, ln_g, ln_b, w_s, b_s_t, tm=512):
    m, gwid = cu.shape
    ng, ch, _ = w_s.shape
    assert m % tm == 0 and tm % ch == 0
    row = pl.BlockSpec((tm, gwid), lambda i: (i, 0))
    vec = pl.BlockSpec((1, gwid), lambda i: (0, 0))
    return pl.pallas_call(
        _gmlp_kernel,
        grid=(m // tm,),
        in_specs=[row, row, vec, vec,
                  pl.BlockSpec((ng, ch, ch), lambda i: (0, 0, 0)),
                  pl.BlockSpec((ch, ng), lambda i: (0, 0))],
        out_specs=row,
        out_shape=jax.ShapeDtypeStruct((m, gwid), BF16),
        compiler_params=_params(("parallel",)),
        name="gmlp_prompt",
    )(cu, cv, ln_g, ln_b, w_s, b_s_t)


def _s5_kernel(u_ref, bm_ref, a_ref, cm_ref, d_ref, gw_ref, gb_ref, o_ref, hl_ref, bu_ref, hh_ref, hs_ref):
    tt = u_ref.shape[1]
    ns = a_ref.shape[1]

    @pl.when(pl.program_id(1) == 0)
    def _():
        hs_ref[...] = jnp.zeros(hs_ref.shape, F32)

    u = u_ref[0]
    bu_ref[...] = jnp.dot(u.astype(BF16), bm_ref[...], preferred_element_type=F32)
    ar = jnp.broadcast_to(a_ref[0:1, :], (SUBLANES, ns))
    ai = jnp.broadcast_to(a_ref[1:2, :], (SUBLANES, ns))
    sub = lax.broadcasted_iota(jnp.int32, (SUBLANES, ns), 0)

    def body(t8, h):
        hr, hi = h
        r0 = pl.multiple_of(t8 * SUBLANES, SUBLANES)
        tile = bu_ref[pl.ds(r0, SUBLANES), :]
        outr = jnp.zeros((SUBLANES, ns), F32)
        outi = jnp.zeros((SUBLANES, ns), F32)
        for s in range(SUBLANES):
            br = jnp.broadcast_to(tile[s:s + 1, :ns], (SUBLANES, ns))
            bi = jnp.broadcast_to(tile[s:s + 1, ns:], (SUBLANES, ns))
            hr, hi = ar * hr - ai * hi + br, ar * hi + ai * hr + bi
            outr = jnp.where(sub == s, hr, outr)
            outi = jnp.where(sub == s, hi, outi)
        hh_ref[pl.ds(r0, SUBLANES), 0:ns] = outr
        hh_ref[pl.ds(r0, SUBLANES), ns:] = outi
        return hr, hi

    hr, hi = lax.fori_loop(0, tt // SUBLANES, body, (hs_ref[:, 0:ns], hs_ref[:, ns:]))
    hs_ref[:, 0:ns] = hr
    hs_ref[:, ns:] = hi
    hl_ref[0] = hs_ref[...]
    y = jnp.dot(hh_ref[...].astype(BF16), cm_ref[...], preferred_element_type=F32) + d_ref[...] * u
    y = _gelu(y)
    z = jnp.dot(y.astype(BF16), gw_ref[...], preferred_element_type=F32) + gb_ref[...]
    o_ref[0] = (y * _sigmoid(z)).astype(o_ref.dtype)


def _s5_prompt(du, bmat, a_ri, cmat, d_skip, glu_w, glu_b, tt=256):
    b, t, gwid = du.shape
    ns2 = bmat.shape[1]
    const = lambda shape: pl.BlockSpec(shape, lambda bi, ti: (0,) * len(shape))
    return pl.pallas_call(
        _s5_kernel,
        grid=(b, t // tt),
        in_specs=[pl.BlockSpec((1, tt, gwid), lambda bi, ti: (bi, ti, 0)),
                  const(bmat.shape), const(a_ri.shape), const(cmat.shape), const(d_skip.shape),
                  const(glu_w.shape), const(glu_b.shape)],
        out_specs=[pl.BlockSpec((1, tt, gwid), lambda bi, ti: (bi, ti, 0)),
                   pl.BlockSpec((1, SUBLANES, ns2), lambda bi, ti: (bi, 0, 0))],
        out_shape=[jax.ShapeDtypeStruct((b, t, gwid), BF16),
                   jax.ShapeDtypeStruct((b, SUBLANES, ns2), F32)],
        scratch_shapes=[pltpu.VMEM((tt, ns2), F32), pltpu.VMEM((tt, ns2), F32),
                        pltpu.VMEM((SUBLANES, ns2), F32)],
        compiler_params=_params(("parallel", "arbitrary")),
        name="s5_prompt",
    )(du, bmat, a_ri, cmat, d_skip, glu_w, glu_b)


def _outproj_kernel(x_ref, a_ref, b_ref, c_ref, d_ref, w_ref, o_ref):
    acc = x_ref[...]
    kw = a_ref.shape[1]
    for p, r in enumerate((a_ref, b_ref, c_ref, d_ref)):
        acc = acc + jnp.dot(r[...], w_ref[p * kw:(p + 1) * kw, :], preferred_element_type=F32)
    o_ref[...] = acc


def _outproj(x2d, parts, w_bf, tm):
    m, d = x2d.shape
    kw = parts[0].shape[1]
    part = pl.BlockSpec((tm, kw), lambda i: (i, 0))
    row = pl.BlockSpec((tm, d), lambda i: (i, 0))
    return pl.pallas_call(
        _outproj_kernel,
        grid=(m // tm,),
        in_specs=[row, part, part, part, part,
                  pl.BlockSpec(w_bf.shape, lambda i: (0, 0))],
        out_specs=row,
        out_shape=jax.ShapeDtypeStruct((m, d), F32),
        compiler_params=_params(("parallel",)),
        name="outproj",
    )(x2d, *parts, w_bf)


def _ffn_kernel(x_ref, g_ref, wg_ref, wu_ref, wd_ref, o_ref, h_ref):
    @pl.when(pl.program_id(1) == 0)
    def _():
        x = x_ref[...]
        ms = jnp.mean(x * x, axis=-1, keepdims=True)
        h_ref[...] = (x * lax.rsqrt(ms + RMS_EPS) * g_ref[...]).astype(BF16)
        o_ref[...] = x

    h = h_ref[...]
    gate = jnp.dot(h, wg_ref[...], preferred_element_type=F32)
    up = jnp.dot(h, wu_ref[...], preferred_element_type=F32)
    act = (gate * _sigmoid(gate) * up).astype(BF16)
    o_ref[...] += jnp.dot(act, wd_ref[...], preferred_element_type=F32)


def _ffn(x2d, g, wg, wu, wd, tm, tf=512):
    m, d = x2d.shape
    f = wg.shape[1]
    assert f % tf == 0
    row = pl.BlockSpec((tm, d), lambda i, j: (i, 0))
    return pl.pallas_call(
        _ffn_kernel,
        grid=(m // tm, f // tf),
        in_specs=[row, pl.BlockSpec((1, d), lambda i, j: (0, 0)),
                  pl.BlockSpec((d, tf), lambda i, j: (0, j)),
                  pl.BlockSpec((d, tf), lambda i, j: (0, j)),
                  pl.BlockSpec((tf, d), lambda i, j: (j, 0))],
        out_specs=row,
        out_shape=jax.ShapeDtypeStruct((m, d), F32),
        scratch_shapes=[pltpu.VMEM((tm, d), BF16)],
        compiler_params=_params(("parallel", "arbitrary")),
        name="ffn",
    )(x2d, g, wg, wu, wd)


def _rmsnorm_kernel(x_ref, g_ref, o_ref):
    x = x_ref[...]
    ms = jnp.mean(x * x, axis=-1, keepdims=True)
    o_ref[...] = x * lax.rsqrt(ms + RMS_EPS) * g_ref[...]


def _rmsnorm(x2d, g, tm):
    m, d = x2d.shape
    row = pl.BlockSpec((tm, d), lambda i: (i, 0))
    return pl.pallas_call(
        _rmsnorm_kernel,
        grid=(m // tm,),
        in_specs=[row, pl.BlockSpec((1, d), lambda i: (0, 0))],
        out_specs=row,
        out_shape=jax.ShapeDtypeStruct((m, d), F32),
        compiler_params=_params(("parallel",)),
        name="final_norm",
    )(x2d, g)


def _page_spec(width, n_pool, pages_per_step, r):
    def index(b, s, lay_ref, pt_ref):
        return (lay_ref[0] * n_pool + pt_ref[b, s * pages_per_step + r], 0, 0)
    return pl.BlockSpec((1, LANES, width), index)


def _moba_kmean_kernel(lay_ref, pt_ref, *refs, pages_per_block):
    o_ref = refs[-1]
    k_refs = refs[:-1]
    inv = 1.0 / (pages_per_block * k_refs[0].shape[1])
    for n in range(len(k_refs) // pages_per_block):
        acc = None
        for r in range(pages_per_block):
            ks = jnp.sum(k_refs[n * pages_per_block + r][0], axis=0, keepdims=True)
            acc = ks if acc is None else acc + ks
        o_ref[0, n:n + 1, :] = acc * inv


def _moba_kmean(layer, page_table, cache_k, pages_per_step=16):
    n_layers_pool, page, width = cache_k.shape
    bd, n_pages = page_table.shape
    ppb = MOBA_BLOCK // page
    n_pool = n_layers_pool // layer[1]
    bps = pages_per_step // ppb
    assert n_pages % pages_per_step == 0 and bps % SUBLANES == 0 and page == LANES
    grid_spec = pltpu.PrefetchScalarGridSpec(
        num_scalar_prefetch=2,
        grid=(bd, n_pages // pages_per_step),
        in_specs=[_page_spec(width, n_pool, pages_per_step, r) for r in range(pages_per_step)],
        out_specs=pl.BlockSpec((1, bps, width), lambda b, s, lay, pt: (b, s, 0)),
    )
    return pl.pallas_call(
        functools.partial(_moba_kmean_kernel, pages_per_block=ppb),
        grid_spec=grid_spec,
        out_shape=jax.ShapeDtypeStruct((bd, n_pages // ppb, width), F32),
        compiler_params=_params(("parallel", "arbitrary")),
        name="moba_dec_kmean",
    )(layer[0], page_table, *([cache_k] * pages_per_step))


def _moba_topk_kernel(km_ref, q_ref, o_ref, *, heads):
    km = km_ref[0]
    nblk, width = km.shape
    hd = width // heads
    prod = km * q_ref[0]
    blk = lax.broadcasted_iota(jnp.int32, (nblk, 1), 0)
    lane = lax.broadcasted_iota(jnp.int32, (1, LANES), 1)
    rows = []
    for h in range(heads):
        g = jnp.sum(prod[:, h * hd:(h + 1) * hd], axis=1, keepdims=True)
        row = jnp.zeros((1, LANES), jnp.int32)
        for r in range(MOBA_TOPK):
            mx = jnp.max(g, axis=0, keepdims=True)
            ix = jnp.min(jnp.where(g == mx, blk, nblk), axis=0, keepdims=True)
            row = jnp.where(lane == r, ix, row)
            g = jnp.where(blk == ix, -jnp.inf, g)
        rows.append(row)
    rows.append(jnp.zeros((SUBLANES - heads, LANES), jnp.int32))
    o_ref[0] = jnp.concatenate(rows, axis=0)


def _moba_topk(kmean, q, heads):
    bd, nblk, width = kmean.shape
    assert nblk >= MOBA_TOPK and heads <= SUBLANES
    out = pl.pallas_call(
        functools.partial(_moba_topk_kernel, heads=heads),
        grid=(bd,),
        in_specs=[pl.BlockSpec((1, nblk, width), lambda b: (b, 0, 0)),
                  pl.BlockSpec((1, 1, width), lambda b: (b, 0, 0))],
        out_specs=pl.BlockSpec((1, SUBLANES, LANES), lambda b: (b, 0, 0)),
        out_shape=jax.ShapeDtypeStruct((bd, SUBLANES, LANES), jnp.int32),
        compiler_params=_params(("parallel",)),
        name="moba_dec_topk",
    )(kmean, q)
    return out[:, :heads, :MOBA_TOPK]


def _moba_dec_attn_kernel(lay_ref, pt_ref, ix_ref, q_ref, kn_ref, vn_ref, *refs):
    o_ref = refs[-1]
    n = (len(refs) - 1) // 2
    k_refs, v_refs = refs[:n], refs[n:2 * n]
    hd = q_ref.shape[2]
    q = q_ref[0] * (1.0 / math.sqrt(hd))
    s_self = jnp.sum(q * kn_ref[0], axis=1, keepdims=True)
    ss = [jnp.sum(kr[0] * q, axis=1, keepdims=True) for kr in k_refs]
    m = s_self
    for s in ss:
        m = jnp.maximum(m, jnp.max(s, axis=0, keepdims=True))
    l = jnp.exp(s_self - m)
    acc = l * vn_ref[0]
    for s, vr in zip(ss, v_refs):
        p = jnp.exp(s - m)
        l = l + jnp.sum(p, axis=0, keepdims=True)
        acc = acc + jnp.sum(p * vr[0], axis=0, keepdims=True)
    o_ref[0] = (acc / l).astype(o_ref.dtype)


def _moba_dec_attn(layer, page_table, top_idx, q, k_new, v_new, cache_k, cache_v, heads):
    n_layers_pool, page, width = cache_k.shape
    bd = page_table.shape[0]
    hd = width // heads
    ppb = MOBA_BLOCK // page
    n_pool = n_layers_pool // layer[1]
    per_b = heads * MOBA_TOPK

    def page_spec(r, half):
        def index(b, h, lay_ref, pt_ref, ix_ref):
            blk = ix_ref[b * per_b + h * MOBA_TOPK + r]
            return (lay_ref[0] * n_pool + pt_ref[b, blk * ppb + half], 0, h)
        return pl.BlockSpec((1, page, hd), index)

    pages = [page_spec(r, half) for r in range(MOBA_TOPK) for half in range(ppb)]
    tok = pl.BlockSpec((1, 1, hd), lambda b, h, lay, pt, ix: (b, 0, h))
    grid_spec = pltpu.PrefetchScalarGridSpec(
        num_scalar_prefetch=3,
        grid=(bd, heads),
        in_specs=[tok, tok, tok] + pages + pages,
        out_specs=tok,
    )
    n = len(pages)
    return pl.pallas_call(
        _moba_dec_attn_kernel,
        grid_spec=grid_spec,
        out_shape=jax.ShapeDtypeStruct((bd, 1, width), BF16),
        compiler_params=_params(("parallel", "parallel")),
        name="moba_dec_attn",
    )(layer[0], page_table, top_idx.reshape(-1), q, k_new, v_new, *([cache_k] * n), *([cache_v] * n))


def _diff_dec_kernel(lay_ref, pt_ref, l_ref, li_ref, g_ref, q_ref, kn_ref, vn_ref, *refs, qk_heads):
    o_ref, m_ref, s_ref, acc_ref = refs[-4:]
    n = (len(refs) - 4) // 2
    k_refs, v_refs = refs[:n], refs[n:2 * n]
    width = q_ref.shape[2]
    qd = width // qk_heads
    vd = 2 * qd
    step = pl.program_id(1)
    rowj = lax.broadcasted_iota(jnp.int32, (qk_heads, width), 0)
    lanej = lax.broadcasted_iota(jnp.int32, (qk_heads, width), 1) // qd
    qrows = jnp.where(rowj == lanej, q_ref[0] * (1.0 / math.sqrt(qd)), 0.0)

    @pl.when(step == 0)
    def _():
        m_ref[...] = jnp.sum(qrows * kn_ref[0], axis=1, keepdims=True)
        s_ref[...] = jnp.ones(s_ref.shape, F32)
        acc_ref[...] = jnp.broadcast_to(vn_ref[0], acc_ref.shape)

    qb = qrows.astype(BF16)
    s = jnp.concatenate(
        [lax.dot_general(qb, kr[0].astype(BF16), _NT, preferred_element_type=F32) for kr in k_refs], axis=1)
    m_old = m_ref[...]
    m_new = jnp.maximum(m_old, jnp.max(s, axis=1, keepdims=True))
    alpha = jnp.exp(m_old - m_new)
    p = jnp.exp(s - m_new)
    s_ref[...] = alpha * s_ref[...] + jnp.sum(p, axis=1, keepdims=True)
    page = k_refs[0].shape[1]
    pv = None
    for r, vr in enumerate(v_refs):
        d = jnp.dot(p[:, r * page:(r + 1) * page].astype(BF16), vr[0].astype(BF16), preferred_element_type=F32)
        pv = d if pv is None else pv + d
    acc_ref[...] = alpha * acc_ref[...] + pv
    m_ref[...] = m_new

    @pl.when(step == pl.num_programs(1) - 1)
    def _():
        o = acc_ref[...] / s_ref[...]
        lam = _diff_lambda(l_ref, li_ref)
        for h in range(qk_heads // 2):
            cs = slice(h * vd, (h + 1) * vd)
            d = o[2 * h:2 * h + 1, cs] - lam * o[2 * h + 1:2 * h + 2, cs]
            o_ref[0, :, cs] = _subln(d, g_ref[...], li_ref[...]).astype(o_ref.dtype)


def _diff_dec(layer, page_table, lam_p, lam_init, subln_g, q, k_new, v_new, cache_k, cache_v, qk_heads,
              pages_per_step=16):
    n_layers_pool, page, width = cache_k.shape
    bd, n_pages = page_table.shape
    n_pool = n_layers_pool // layer[1]
    assert n_pages % pages_per_step == 0 and page == LANES
    pages = [_page_spec(width, n_pool, pages_per_step, r) for r in range(pages_per_step)]
    tok = pl.BlockSpec((1, 1, width), lambda b, s, lay, pt: (b, 0, 0))
    const = lambda shape: pl.BlockSpec(shape, lambda b, s, lay, pt: (0,) * len(shape))
    grid_spec = pltpu.PrefetchScalarGridSpec(
        num_scalar_prefetch=2,
        grid=(bd, n_pages // pages_per_step),
        in_specs=[const(lam_p.shape), const((1, 1)), const(subln_g.shape), tok, tok, tok] + pages + pages,
        out_specs=tok,
        scratch_shapes=[pltpu.VMEM((qk_heads, 1), F32), pltpu.VMEM((qk_heads, 1), F32),
                        pltpu.VMEM((qk_heads, width), F32)],
    )
    return pl.pallas_call(
        functools.partial(_diff_dec_kernel, qk_heads=qk_heads),
        grid_spec=grid_spec,
        out_shape=jax.ShapeDtypeStruct((bd, 1, width), BF16),
        compiler_params=_params(("parallel", "arbitrary")),
        name="diff_dec",
    )(layer[0], page_table, lam_p, lam_init, subln_g, q, k_new, v_new,
      *([cache_k] * pages_per_step), *([cache_v] * pages_per_step))


def _sample_mix_kernel(cu_ref, cv_ref, lg_ref, lb_ref, w0_ref, b0_ref, du_ref, h_ref, a_ref, bm_ref, cm_ref,
                       d_ref, gw_ref, gb_ref, co_ref, cvn_ref, do_ref, hn_ref):
    u = _gelu(cu_ref[...])
    vn = _layernorm(_gelu(cv_ref[...]), lg_ref[...], lb_ref[...])
    cvn_ref[...] = vn
    co_ref[...] = (u * (vn * w0_ref[...] + b0_ref[...])).astype(co_ref.dtype)
    ns = a_ref.shape[1]
    du = du_ref[...]
    bu = jnp.dot(du, bm_ref[...], precision=HIGHEST, preferred_element_type=F32)
    h = h_ref[...]
    hr, hi = h[:, :ns], h[:, ns:]
    ar, ai = a_ref[0:1, :], a_ref[1:2, :]
    hn = jnp.concatenate([ar * hr - ai * hi + bu[:, :ns], ar * hi + ai * hr + bu[:, ns:]], axis=1)
    hn_ref[...] = hn
    y = jnp.dot(hn, cm_ref[...], precision=HIGHEST, preferred_element_type=F32) + d_ref[...] * du
    y = _gelu(y)
    z = jnp.dot(y, gw_ref[...], precision=HIGHEST, preferred_element_type=F32) + gb_ref[...]
    do_ref[...] = (y * _sigmoid(z)).astype(do_ref.dtype)


def _sample_mix(cu, cv, ln_g, ln_b, w0, b0, du, h0, a_ri, bmat, cmat, d_skip, glu_w, glu_b):
    bd, gwid = cu.shape
    return pl.pallas_call(
        _sample_mix_kernel,
        out_shape=[jax.ShapeDtypeStruct((bd, gwid), BF16), jax.ShapeDtypeStruct((bd, gwid), F32),
                   jax.ShapeDtypeStruct((bd, gwid), BF16), jax.ShapeDtypeStruct(h0.shape, F32)],
        compiler_params=pltpu.CompilerParams(vmem_limit_bytes=VMEM_LIMIT),
        name="sample_mix",
    )(cu, cv, ln_g, ln_b, w0, b0, du, h0, a_ri, bmat, cmat, d_skip, glu_w, glu_b)


def _rope_tables(pos, head_dim):
    half = head_dim // 2
    inv = jnp.power(jnp.float32(ROPE_THETA), -jnp.arange(half, dtype=F32) * 2.0 / head_dim)
    ang = pos.astype(F32)[:, None] * inv[None, :]
    cos, sin = jnp.cos(ang), jnp.sin(ang)
    reps = LANES // head_dim
    return (jnp.tile(jnp.concatenate([cos, cos], axis=1), (1, reps)),
            jnp.tile(jnp.concatenate([-sin, sin], axis=1), (1, reps)))


def _block_diag(blocks):
    g, r, c = blocks.shape
    eye = jnp.eye(g, dtype=blocks.dtype)
    return jnp.einsum('grc,gh->grhc', blocks, eye).reshape(g * r, g * c)


def _s5_matrices(a_re, a_im, b_re, b_im, c_re, c_im, log_step):
    lam = lax.complex(a_re.astype(F32), a_im.astype(F32))
    step = jnp.exp(log_step.astype(F32))[:, None]
    a_bar = jnp.exp(lam * step)
    b_bar = ((a_bar - 1.0) / lam)[..., None] * lax.complex(b_re.astype(F32), b_im.astype(F32))
    a_ri = jnp.stack([a_bar.real.reshape(-1), a_bar.imag.reshape(-1)])
    bt = jnp.swapaxes(b_bar, 1, 2)
    bmat = jnp.concatenate([_block_diag(bt.real), _block_diag(bt.imag)], axis=1)
    ct_re = jnp.swapaxes(c_re.astype(F32), 1, 2)
    ct_im = jnp.swapaxes(c_im.astype(F32), 1, 2)
    cmat = jnp.concatenate([_block_diag(ct_re), _block_diag(-ct_im)], axis=0)
    return a_ri, bmat, cmat


def _row_tile(m, want):
    return want if m % want == 0 else m


def kernel(x_prompt, x_sample, cache_a_k, cache_a_v, cache_b_k, cache_b_v, state_ssm, page_table, norm_mix_g, w_in, w_out, diff_lam_q1, diff_lam_k1, diff_lam_q2, diff_lam_k2, diff_subln_g, gmlp_ln_g, gmlp_ln_b, gmlp_w_s, gmlp_b_s, ssm_a_re, ssm_a_im, ssm_b_re, ssm_b_im, ssm_c_re, ssm_c_im, ssm_d, ssm_log_step, glu_w, glu_b, norm_ffn_g, ffn_w_gate, ffn_w_up, ffn_w_down, final_norm_g):
    bp, t, d = x_prompt.shape
    bd, ts, _ = x_sample.shape
    depth, n_pool, page, a_heads, a_hd = cache_a_k.shape
    b_qk_heads, b_qd = cache_b_k.shape[3], cache_b_k.shape[4]
    b_heads = cache_b_v.shape[3]
    gw = a_heads * a_hd
    n_pages = page_table.shape[1]
    past_len = n_pages * page
    d_groups, d_state = ssm_a_re.shape[1], ssm_a_re.shape[2]
    ns = d_groups * d_state
    assert ts == 1 and past_len % MOBA_BLOCK == 0 and past_len % gmlp_w_s.shape[2] == 0
    mp, ms = bp * t, bd * ts

    pos_p = jnp.arange(t, dtype=jnp.int32)
    pos_s = jnp.full((ms,), past_len, jnp.int32)
    tabs_p = _rope_tables(pos_p, a_hd) + _rope_tables(pos_p, b_qd)
    tabs_s = _rope_tables(pos_s, a_hd) + _rope_tables(pos_s, b_qd)

    pool = depth * n_pool
    ck_a = cache_a_k.reshape(pool, page, gw)
    cv_a = cache_a_v.reshape(pool, page, gw)
    ck_b = cache_b_k.reshape(pool, page, gw)
    cv_b = cache_b_v.reshape(pool, page, gw)

    tm_p = _row_tile(mp, 512)
    xp = x_prompt.reshape(mp, d)
    xs = x_sample.reshape(ms, d)
    outs = {k: [] for k in ("akp", "avp", "aks", "avs", "bkp", "bvp", "bks", "bvs", "hp", "hs", "cvs")}
    for i in range(depth):
        lam_init = jnp.full((1, 1), 0.8 - 0.6 * math.exp(-0.3 * i), F32)
        layer = (jnp.full((1,), i, jnp.int32), depth)
        g_mix = norm_mix_g[i].reshape(1, d)
        w_in_bf = w_in[i].astype(BF16)
        w_out_bf = w_out[i].astype(BF16)
        wg_bf, wu_bf, wd_bf = ffn_w_gate[i].astype(BF16), ffn_w_up[i].astype(BF16), ffn_w_down[i].astype(BF16)
        g_ffn = norm_ffn_g[i].reshape(1, d)
        lam_p = jnp.stack([diff_lam_q1[i], diff_lam_k1[i], diff_lam_q2[i], diff_lam_k2[i]]).astype(F32)
        subln_g = diff_subln_g[i].reshape(1, -1)
        ln_g, ln_b = gmlp_ln_g[i].reshape(1, gw), gmlp_ln_b[i].reshape(1, gw)
        a_ri, bmat, cmat = _s5_matrices(ssm_a_re[i], ssm_a_im[i], ssm_b_re[i], ssm_b_im[i],
                                        ssm_c_re[i], ssm_c_im[i], ssm_log_step[i])
        d_skip = ssm_d[i].reshape(1, gw)
        glu_b_i = glu_b[i].reshape(1, gw)

        aq, ak, av, bq, bk, bv, cu, cv, du = _proj_in(xp, g_mix, w_in_bf, tabs_p, tm_p, gw, a_hd // 2, b_qd // 2)
        r3 = lambda z: z.reshape(bp, t, gw)
        a_out = _moba_prompt(r3(aq), r3(ak), r3(av), a_heads)
        b_out = _diff_prompt(lam_p, lam_init, subln_g, r3(bq), r3(bk), r3(bv), b_heads)
        c_out = _gmlp_prompt(cu, cv, ln_g, ln_b, gmlp_w_s[i], gmlp_b_s[i].T)
        d_out, h_last = _s5_prompt(r3(du), bmat.astype(BF16), a_ri, cmat.astype(BF16), d_skip,
                                   glu_w[i].astype(BF16), glu_b_i)
        x1 = _outproj(xp, (a_out.reshape(mp, gw), b_out.reshape(mp, gw), c_out, d_out.reshape(mp, gw)),
                      w_out_bf, tm_p)
        xp = _ffn(x1, g_ffn, wg_bf, wu_bf, wd_bf, tm_p)
        outs["akp"].append(ak.reshape(bp, t, a_heads, a_hd))
        outs["avp"].append(av.reshape(bp, t, a_heads, a_hd))
        outs["bkp"].append(bk.reshape(bp, t, b_qk_heads, b_qd))
        outs["bvp"].append(bv.reshape(bp, t, b_heads, gw // b_heads))
        hl = h_last[:, 0, :]
        outs["hp"].append(jnp.stack([hl[:, :ns].reshape(bp, d_groups, d_state),
                                     hl[:, ns:].reshape(bp, d_groups, d_state)], axis=-1))

        aq, ak, av, bq, bk, bv, cu, cv, du = _proj_in(xs, g_mix, w_in_bf, tabs_s, ms, gw, a_hd // 2, b_qd // 2)
        s3 = lambda z: z.reshape(bd, 1, gw)
        kmean = _moba_kmean(layer, page_table, ck_a)
        top_idx = _moba_topk(kmean, s3(aq), a_heads)
        a_out = _moba_dec_attn(layer, page_table, top_idx, s3(aq), s3(ak), s3(av), ck_a, cv_a, a_heads)
        b_out = _diff_dec(layer, page_table, lam_p, lam_init, subln_g, s3(bq), s3(bk), s3(bv), ck_b, cv_b,
                          b_qk_heads)
        h0 = jnp.concatenate([state_ssm[i, ..., 0].reshape(bd, ns), state_ssm[i, ..., 1].reshape(bd, ns)],
                             axis=1).astype(F32)
        w0 = jnp.repeat(gmlp_w_s[i][:, 0, 0], gw // gmlp_w_s.shape[1]).reshape(1, gw)
        b0 = jnp.repeat(gmlp_b_s[i][:, 0], gw // gmlp_b_s.shape[1]).reshape(1, gw)
        c_out, c_v, d_out, h_new = _sample_mix(cu, cv, ln_g, ln_b, w0, b0, du, h0, a_ri, bmat, cmat, d_skip,
                                               glu_w[i].astype(F32), glu_b_i)
        x1 = _outproj(xs, (a_out.reshape(ms, gw), b_out.reshape(ms, gw), c_out, d_out), w_out_bf, ms)
        xs = _ffn(x1, g_ffn, wg_bf, wu_bf, wd_bf, ms)
        outs["aks"].append(ak.reshape(bd, ts, a_heads, a_hd))
        outs["avs"].append(av.reshape(bd, ts, a_heads, a_hd))
        outs["bks"].append(bk.reshape(bd, ts, b_qk_heads, b_qd))
        outs["bvs"].append(bv.reshape(bd, ts, b_heads, gw // b_heads))
        outs["cvs"].append(c_v.reshape(bd, ts, gw))
        outs["hs"].append(jnp.stack([h_new[:, :ns].reshape(bd, d_groups, d_state),
                                     h_new[:, ns:].reshape(bd, d_groups, d_state)], axis=-1))

    g_fin = final_norm_g.reshape(1, d)
    y_prompt = _rmsnorm(xp, g_fin, tm_p).reshape(bp, t, d)
    y_sample = _rmsnorm(xs, g_fin, ms).reshape(bd, ts, d)
    st = lambda k: jnp.stack(outs[k])
    return (y_prompt, y_sample, st("akp"), st("avp"), st("aks"), st("avs"),
            st("bkp"), st("bvp"), st("bks"), st("bvs"), st("hp"), st("hs"), st("cvs"))
```
